```python
import math
import jax, jax.numpy as jnp
from jax import lax
import numpy as np

D_MODEL = 2048
BATCH = 1
SEQ = 16384
DEPTH = 1

CHUNK = 64
N_META = 16
HEAD_DIM = 128
SB_HEADS = 8
FOX_HEADS = 8
SB_WIDTH = SB_HEADS * HEAD_DIM
FOX_WIDTH = FOX_HEADS * HEAD_DIM
MIX_WIDTH = SB_WIDTH + FOX_WIDTH
C_IN = 3 * SB_WIDTH + 4 * FOX_WIDTH + FOX_HEADS
Q_BLOCK = 128
PEER_HEADS = 8
PEER_TOPK = 16
N_KEYS = 128
N_EXPERTS = N_KEYS * N_KEYS
D_KEY = 256
PEER_BLOCK = 128
EPS = 1e-6

kernel_name = "hymba_sbfox_peer_block"


def rms_norm(x, gain):
    xf = x.astype(jnp.float32)
    y = xf * lax.rsqrt(jnp.mean(xf * xf, axis=-1, keepdims=True) + EPS)
    return y.astype(x.dtype) * gain


def to_query_blocks(a):
    b, l = a.shape[:2]
    return jnp.moveaxis(a.reshape((b, l // Q_BLOCK, Q_BLOCK) + a.shape[2:]), 1, 0)


def from_query_blocks(a):
    a = jnp.moveaxis(a, 0, 1)
    b, nb, qb = a.shape[:3]
    return a.reshape(b, nb * qb, -1)


def stick_breaking_attention(q, k, v):
    L = k.shape[1]
    k_pos = jnp.arange(L)
    scale = 1.0 / math.sqrt(HEAD_DIM)
    starts = jnp.arange(L // Q_BLOCK, dtype=jnp.int32) * Q_BLOCK

    def block(args):
        qb, start = args
        q_pos = start + jnp.arange(Q_BLOCK)
        z = jnp.einsum('bqhd,bkhd->bhqk', qb, k).astype(jnp.float32) * scale
        visible = k_pos[None, :] < q_pos[:, None]
        log_1mb = jnp.where(visible, jax.nn.log_sigmoid(-z), 0.0)
        after = lax.cumsum(log_1mb, axis=3, reverse=True) - log_1mb
        w = jnp.where(visible, jnp.exp(jax.nn.log_sigmoid(z) + after), 0.0)
        return jnp.einsum('bhqk,bkhd->bqhd', w.astype(v.dtype), v)

    out = lax.map(block, (to_query_blocks(q), starts))
    return from_query_blocks(out)


def forgetting_attention(q, k, v, log_f):
    L = k.shape[1]
    k_pos = jnp.arange(L)
    scale = 1.0 / math.sqrt(HEAD_DIM)
    cum_f = lax.cumsum(log_f, axis=1)
    cum_f_k = jnp.transpose(cum_f, (0, 2, 1))
    starts = jnp.arange(L // Q_BLOCK, dtype=jnp.int32) * Q_BLOCK

    def block(args):
        qb, fq, start = args
        q_pos = start + jnp.arange(Q_BLOCK)
        z = jnp.einsum('bqhd,bkhd->bhqk', qb, k).astype(jnp.float32) * scale
        logits = z + jnp.transpose(fq, (0, 2, 1))[..., None] - cum_f_k[:, :, None, :]
        visible = k_pos[None, :] <= q_pos[:, None]
        p = jax.nn.softmax(jnp.where(visible, logits, -jnp.inf), axis=-1)
        return jnp.einsum('bhqk,bkhd->bqhd', p.astype(v.dtype), v)

    out = lax.map(block, (to_query_blocks(q), to_query_blocks(cum_f), starts))
    return from_query_blocks(out)


def mixer_sublayer(h, w_in, b_forget, fox_q_gain, fox_k_gain, sb_out_gain, fox_out_gain, w_out):
    B, L, _ = h.shape
    proj = h @ w_in
    s3 = 3 * SB_WIDTH
    sb_q, sb_k, sb_v, fq, fk, fv, fgate, flogit = jnp.split(
        proj,
        [SB_WIDTH, 2 * SB_WIDTH, s3, s3 + FOX_WIDTH, s3 + 2 * FOX_WIDTH,
         s3 + 3 * FOX_WIDTH, s3 + 4 * FOX_WIDTH],
        axis=-1)

    def heads(a, n):
        return a.reshape(B, L, n, HEAD_DIM)

    sb = stick_breaking_attention(heads(sb_q, SB_HEADS), heads(sb_k, SB_HEADS), heads(sb_v, SB_HEADS))
    log_f = jax.nn.log_sigmoid((flogit + b_forget).astype(jnp.float32))
    fox = forgetting_attention(rms_norm(heads(fq, FOX_HEADS), fox_q_gain),
                               rms_norm(heads(fk, FOX_HEADS), fox_k_gain),
                               heads(fv, FOX_HEADS), log_f)
    merged = jnp.concatenate(
        [rms_norm(sb, sb_out_gain),
         rms_norm(fox, fox_out_gain) * jax.nn.sigmoid(fgate)], axis=-1)
    return merged @ w_out


def peer_sublayer(h, w_query, sub_keys, u, v):
    B, L, D = h.shape
    blocks = h.reshape(-1, PEER_BLOCK, D)

    def block(hb):
        q = (hb @ w_query).reshape(PEER_BLOCK, PEER_HEADS, 2, D_KEY // 2)
        s = jnp.einsum('thpc,hpnc->thpn', q, sub_keys).astype(jnp.float32)
        top_s, top_i = lax.top_k(s, PEER_TOPK)
        cand_s = (top_s[:, :, 0, :, None] + top_s[:, :, 1, None, :]).reshape(PEER_BLOCK, PEER_HEADS, -1)
        cand_i = (top_i[:, :, 0, :, None] * N_KEYS + top_i[:, :, 1, None, :]).reshape(PEER_BLOCK, PEER_HEADS, -1)
        best_s, best_j = lax.top_k(cand_s, PEER_TOPK)
        idx = jnp.take_along_axis(cand_i, best_j, axis=-1)
        g = jax.nn.softmax(best_s, axis=-1)
        u_sel = jnp.take(u, idx, axis=0)
        a = jax.nn.gelu(jnp.einsum('thkd,td->thk', u_sel, hb), approximate=False)
        v_sel = jnp.take(v, idx, axis=0)
        return jnp.einsum('thk,thkd->td', (g * a).astype(hb.dtype), v_sel)

    return lax.map(block, blocks).reshape(B, L, D)


def setup_inputs(seed: int = 0) -> dict:
    key = jax.random.key(seed)
    ks = jax.random.split(key, 15)
    f32 = jnp.float32
    nrm = lambda k, shape, s: jax.random.normal(k, shape, f32) * s
    gain = lambda k, shape: 1.0 + 0.02 * jax.random.normal(k, shape, f32)
    return {
        "x": nrm(ks[0], (BATCH, SEQ, D_MODEL), 1.0),
        "meta_tokens": nrm(ks[1], (N_META, D_MODEL), 1.0),
        "norm_mix": gain(ks[2], (DEPTH, D_MODEL)),
        "w_in": nrm(ks[3], (DEPTH, D_MODEL, C_IN), D_MODEL ** -0.5),
        "b_forget": jax.random.uniform(ks[4], (DEPTH, FOX_HEADS), f32, 1.0, 4.0),
        "fox_q_gain": gain(ks[5], (DEPTH, HEAD_DIM)),
        "fox_k_gain": gain(ks[6], (DEPTH, HEAD_DIM)),
        "sb_out_gain": gain(ks[7], (DEPTH, SB_WIDTH)),
        "fox_out_gain": gain(ks[8], (DEPTH, FOX_WIDTH)),
        "w_out": nrm(ks[9], (DEPTH, MIX_WIDTH, D_MODEL), MIX_WIDTH ** -0.5),
        "norm_ffn": gain(ks[10], (DEPTH, D_MODEL)),
        "peer_w_query": nrm(ks[11], (DEPTH, D_MODEL, PEER_HEADS * D_KEY), D_MODEL ** -0.5),
        "peer_sub_keys": nrm(ks[12], (DEPTH, PEER_HEADS, 2, N_KEYS, D_KEY // 2), (D_KEY // 2) ** -0.5),
        "peer_u": nrm(ks[13], (DEPTH, N_EXPERTS, D_MODEL), D_MODEL ** -0.5),
        "peer_v": nrm(ks[14], (DEPTH, N_EXPERTS, D_MODEL), PEER_HEADS ** -0.5),
    }


def reference(x, meta_tokens, norm_mix, w_in, b_forget, fox_q_gain, fox_k_gain,
              sb_out_gain, fox_out_gain, w_out, norm_ffn, peer_w_query, peer_sub_keys,
              peer_u, peer_v):
    B, S, D = x.shape
    l_pad = ((S + N_META + Q_BLOCK - 1) // Q_BLOCK) * Q_BLOCK
    meta = jnp.broadcast_to(meta_tokens[None].astype(x.dtype), (B, N_META, D))
    pad = jnp.zeros((B, l_pad - S - N_META, D), x.dtype)
    h = jnp.concatenate([meta, x, pad], axis=1)
    for l in range(DEPTH):
        h = h + mixer_sublayer(rms_norm(h, norm_mix[l]), w_in[l], b_forget[l],
                               fox_q_gain[l], fox_k_gain[l], sb_out_gain[l],
                               fox_out_gain[l], w_out[l])
        h = h + peer_sublayer(rms_norm(h, norm_ffn[l]), peer_w_query[l],
                              peer_sub_keys[l], peer_u[l], peer_v[l])
    return h[:, N_META:N_META + S]
```

```python
import functools
import math

import numpy as np
import jax
import jax.numpy as jnp
from jax import lax
from jax.experimental import pallas as pl
from jax.experimental.pallas import tpu as pltpu

F32 = jnp.float32
BF16 = jnp.bfloat16

D_MODEL = 2048
N_META = 16
HEAD_DIM = 128
N_HEADS = 8
WIDTH = N_HEADS * HEAD_DIM
N_SEG = 7
PEER_HEADS = 8
TOPK = 16
N_KEYS = 128
N_EXPERTS = N_KEYS * N_KEYS
EPS = 1e-6
SCALE = 1.0 / math.sqrt(HEAD_DIM)

LANES = 128
ROW_BLOCK = 512
SB_TQ = 256
FOX_TQ = 256
FOX_TK = 512
PEER_TB = 512
PEER_EC = 512
VMEM_LIMIT = 56 * 1024 * 1024
ZERO_EXP = 110.0
NEG_BIG = -1e30


def _nt_dot(a, b):
    return lax.dot_general(a, b, (((1,), (1,)), ((), ())), preferred_element_type=F32)


def _log_sigmoid(x):
    return jnp.minimum(x, 0.0) - jnp.log1p(jnp.exp(-jnp.abs(x)))


def _softplus(x):
    return jnp.maximum(x, 0.0) + jnp.log1p(jnp.exp(-jnp.abs(x)))


def _proj_kernel(h_ref, g_ref, w_ref, wf_ref, bf_ref, gq_ref, gk_ref, tri_ref,
                 qkv_ref, cumf_ref, hn_scr, carry_scr):
    i = pl.program_id(0)
    j = pl.program_id(1)
    tm = h_ref.shape[0]

    @pl.when(j == 0)
    def _():
        h = h_ref[...]
        ms = jnp.mean(h * h, axis=-1, keepdims=True)
        hn = ((h * lax.rsqrt(ms + EPS)) * g_ref[...]).astype(BF16)
        hn_scr[...] = hn
        lf = _log_sigmoid(jnp.dot(hn, wf_ref[...], preferred_element_type=F32) + bf_ref[...])
        p0 = lf.astype(BF16)
        r0 = lf - p0.astype(F32)
        p1 = r0.astype(BF16)
        p2 = (r0 - p1.astype(F32)).astype(BF16)
        tri = tri_ref[...]
        c = (jnp.dot(tri, p0, preferred_element_type=F32)
             + jnp.dot(tri, p1, preferred_element_type=F32)
             + jnp.dot(tri, p2, preferred_element_type=F32))

        @pl.when(i == 0)
        def _():
            carry_scr[...] = jnp.zeros_like(carry_scr)

        c = c + carry_scr[0:1, :]
        cumf_ref[...] = c
        carry_scr[...] = jnp.broadcast_to(c[tm - 1:tm, :], carry_scr.shape)

    acc = jnp.dot(hn_scr[...], w_ref[...], preferred_element_type=F32)

    @pl.when(j == 0)
    def _():
        qkv_ref[...] = (acc * SCALE).astype(BF16)

    @pl.when((j == 1) | (j == 2) | (j == 5))
    def _():
        qkv_ref[...] = acc.astype(BF16)

    def _qk_norm(gain, scale):
        for k in range(N_HEADS):
            a = acc[:, k * HEAD_DIM:(k + 1) * HEAD_DIM]
            ms = jnp.mean(a * a, axis=-1, keepdims=True)
            y = (a * lax.rsqrt(ms + EPS)) * gain
            if scale != 1.0:
                y = y * scale
            qkv_ref[:, k * HEAD_DIM:(k + 1) * HEAD_DIM] = y.astype(BF16)

    @pl.when(j == 3)
    def _():
        _qk_norm(gq_ref[...], SCALE)

    @pl.when(j == 4)
    def _():
        _qk_norm(gk_ref[...], 1.0)

    @pl.when(j == 6)
    def _():
        qkv_ref[...] = (1.0 / (1.0 + jnp.exp(-acc))).astype(BF16)


def _proj(h, g, w_main, w_f, b_f, gq, gk):
    lp = h.shape[0]
    tm = ROW_BLOCK
    tri = jnp.asarray(np.tril(np.ones((tm, tm), np.float32)), BF16)
    return pl.pallas_call(
        _proj_kernel,
        grid=(lp // tm, N_SEG),
        in_specs=[
            pl.BlockSpec((tm, D_MODEL), lambda i, j: (i, 0)),
            pl.BlockSpec((1, D_MODEL), lambda i, j: (0, 0)),
            pl.BlockSpec((D_MODEL, WIDTH), lambda i, j: (0, j)),
            pl.BlockSpec((D_MODEL, LANES), lambda i, j: (0, 0)),
            pl.BlockSpec((1, LANES), lambda i, j: (0, 0)),
            pl.BlockSpec((1, HEAD_DIM), lambda i, j: (0, 0)),
            pl.BlockSpec((1, HEAD_DIM), lambda i, j: (0, 0)),
            pl.BlockSpec((tm, tm), lambda i, j: (0, 0)),
        ],
        out_specs=[
            pl.BlockSpec((tm, WIDTH), lambda i, j: (i, j)),
            pl.BlockSpec((tm, LANES), lambda i, j: (i, 0)),
        ],
        out_shape=[
            jax.ShapeDtypeStruct((lp, N_SEG * WIDTH), BF16),
            jax.ShapeDtypeStruct((lp, LANES), F32),
        ],
        scratch_shapes=[pltpu.VMEM((tm, D_MODEL), BF16), pltpu.VMEM((8, LANES), F32)],
        compiler_params=pltpu.CompilerParams(
            dimension_semantics=("arbitrary", "arbitrary"), vmem_limit_bytes=VMEM_LIMIT),
        name="proj",
    )(h, g, w_main, w_f, b_f, gq, gk, tri)


def _sb_kernel(q_ref, k_ref, v_ref, u2_ref, o_ref):
    qi = pl.program_id(1)
    tq = q_ref.shape[0]
    nd = tq // LANES
    q = q_ref[...]
    u2 = u2_ref[...]
    row = lax.broadcasted_iota(jnp.int32, (tq, LANES), 0) + qi * tq
    col = lax.broadcasted_iota(jnp.int32, (tq, LANES), 1)

    def tile(kb, carry, acc, masked):
        start = pl.multiple_of(kb * LANES, LANES)
        k = k_ref[pl.ds(start, LANES), :]
        v = v_ref[pl.ds(start, LANES), :]
        z = _nt_dot(q, k)
        sp = _softplus(z)
        if masked:
            vis = (col + start) < row
            sp = jnp.where(vis, sp, 0.0)
        hi = sp.astype(BF16)
        lo = (sp - hi.astype(F32)).astype(BF16)
        cs = jnp.dot(jnp.concatenate([hi, lo], axis=1), u2, preferred_element_type=F32)
        w = jnp.exp(z - cs[:, :LANES] - carry)
        if masked:
            w = jnp.where(vis, w, 0.0)
        acc = acc + jnp.dot(w.astype(BF16), v, preferred_element_type=F32)
        return carry + cs[:, LANES:], acc

    carry = jnp.zeros((tq, LANES), F32)
    acc = jnp.zeros((tq, HEAD_DIM), F32)
    for d in range(nd):
        carry, acc = tile(qi * nd + (nd - 1 - d), carry, acc, True)

    def cond(s):
        return (s[0] >= 0) & (s[3] < ZERO_EXP)

    def body(s):
        c, a = tile(s[0], s[1], s[2], False)
        return s[0] - 1, c, a, jnp.min(c)

    s = lax.while_loop(cond, body, (qi * nd - 1, carry, acc, jnp.min(carry)))
    o_ref[...] = s[2]


def _sb_attention(qkv, lp):
    tq = SB_TQ
    tri = np.tril(np.ones((LANES, LANES), np.float32))
    half = np.concatenate([tri, np.ones((LANES, LANES), np.float32)], axis=1)
    u2 = jnp.asarray(np.concatenate([half, half], axis=0), BF16)
    return pl.pallas_call(
        _sb_kernel,
        grid=(N_HEADS, lp // tq),
        in_specs=[
            pl.BlockSpec((tq, HEAD_DIM), lambda h, i: (i, h)),
            pl.BlockSpec((lp, HEAD_DIM), lambda h, i: (0, N_HEADS + h)),
            pl.BlockSpec((lp, HEAD_DIM), lambda h, i: (0, 2 * N_HEADS + h)),
            pl.BlockSpec((2 * LANES, 2 * LANES), lambda h, i: (0, 0)),
        ],
        out_specs=pl.BlockSpec((tq, HEAD_DIM), lambda h, i: (i, h)),
        out_shape=jax.ShapeDtypeStruct((lp, WIDTH), F32),
        compiler_params=pltpu.CompilerParams(
            dimension_semantics=("arbitrary", "arbitrary"), vmem_limit_bytes=VMEM_LIMIT),
        name="sb_attn",
    )(qkv, qkv, qkv, u2)


def _fox_kernel(cend_ref, zb_ref, q_ref, k_ref, v_ref, cq_ref, ck_ref, o_ref):
    h = pl.program_id(0)
    qi = pl.program_id(1)
    tq = q_ref.shape[0]
    tk = FOX_TK
    nc = tk // LANES
    q = q_ref[...]
    cq = cq_ref[0]
    row = lax.broadcasted_iota(jnp.int32, (tq, tk), 0) + qi * tq
    col = lax.broadcasted_iota(jnp.int32, (tq, tk), 1)

    def tile(kt, m, l, acc, masked):
        start = pl.multiple_of(kt * tk, tk)
        k = k_ref[pl.ds(start, tk), :]
        v = v_ref[pl.ds(start, tk), :]
        z = _nt_dot(q, k)
        s = jnp.concatenate(
            [z[:, c * LANES:(c + 1) * LANES] - ck_ref[0, pl.ds(kt * nc + c, 1), :] for c in range(nc)],
            axis=1) + cq
        if masked:
            s = jnp.where((col + start) <= row, s, NEG_BIG)
        m_new = jnp.maximum(m, jnp.max(s, axis=1, keepdims=True))
        alpha = jnp.exp(m - m_new)
        p = jnp.exp(s - m_new)
        l = alpha * l + jnp.sum(p, axis=1, keepdims=True)
        acc = acc * alpha + jnp.dot(p.astype(BF16), v, preferred_element_type=F32)
        return m_new, l, acc

    m = jnp.full((tq, 1), NEG_BIG, F32)
    l = jnp.zeros((tq, 1), F32)
    acc = jnp.zeros((tq, HEAD_DIM), F32)
    kd = (qi * tq) // tk
    m, l, acc = tile(kd, m, l, acc, True)

    zb = zb_ref[0]

    def slack(m):
        return jnp.max(zb + cq - m)

    def cond(s):
        kt = s[0]
        return (kt >= 0) & (s[4] - cend_ref[h, jnp.maximum(kt, 0)] > -ZERO_EXP)

    def body(s):
        m, l, acc = tile(s[0], s[1], s[2], s[3], False)
        return s[0] - 1, m, l, acc, slack(m)

    s = lax.while_loop(cond, body, (kd - 1, m, l, acc, slack(m)))
    o_ref[...] = s[3] / s[2]


def _fox_attention(qkv, cumf, zb, lp):
    tq, tk = FOX_TQ, FOX_TK
    cf = cumf[:, :N_HEADS].T
    cq = cf[:, :, None]
    ck = cf.reshape(N_HEADS, lp // LANES, LANES)
    cend = cf[:, tk - 1::tk]
    grid_spec = pltpu.PrefetchScalarGridSpec(
        num_scalar_prefetch=2,
        grid=(N_HEADS, lp // tq),
        in_specs=[
            pl.BlockSpec((tq, HEAD_DIM), lambda h, i, *_: (i, 3 * N_HEADS + h)),
            pl.BlockSpec((lp, HEAD_DIM), lambda h, i, *_: (0, 4 * N_HEADS + h)),
            pl.BlockSpec((lp, HEAD_DIM), lambda h, i, *_: (0, 5 * N_HEADS + h)),
            pl.BlockSpec((1, tq, 1), lambda h, i, *_: (h, i, 0)),
            pl.BlockSpec((1, lp // LANES, LANES), lambda h, i, *_: (h, 0, 0)),
        ],
        out_specs=pl.BlockSpec((tq, HEAD_DIM), lambda h, i, *_: (i, h)),
    )
    return pl.pallas_call(
        _fox_kernel,
        grid_spec=grid_spec,
        out_shape=jax.ShapeDtypeStruct((lp, WIDTH), F32),
        compiler_params=pltpu.CompilerParams(
            dimension_semantics=("arbitrary", "arbitrary"), vmem_limit_bytes=VMEM_LIMIT),
        name="fox_attn",
    )(cend, zb, qkv, qkv, qkv, cq, ck)


def _mix_kernel(sb_ref, fox_ref, gate_ref, h_ref, gs_ref, gf_ref, w_ref, gn_ref, h1_ref, hf_ref):
    def norm(x, g):
        ms = jnp.mean(x * x, axis=-1, keepdims=True)
        return (x * lax.rsqrt(ms + EPS)) * g

    a = norm(sb_ref[...], gs_ref[...]).astype(BF16)
    b = (norm(fox_ref[...], gf_ref[...]) * gate_ref[...].astype(F32)).astype(BF16)
    y = (jnp.dot(a, w_ref[0:WIDTH, :], preferred_element_type=F32)
         + jnp.dot(b, w_ref[WIDTH:2 * WIDTH, :], preferred_element_type=F32))
    h1 = h_ref[...] + y
    h1_ref[...] = h1
    hf_ref[...] = norm(h1, gn_ref[...]).astype(BF16)


def _mix(sb, fox, qkv, h, gs, gf, w_out, gn):
    lp = h.shape[0]
    tm = ROW_BLOCK
    row = lambda i: (i, 0)
    fixed = lambda i: (0, 0)
    return pl.pallas_call(
        _mix_kernel,
        grid=(lp // tm,),
        in_specs=[
            pl.BlockSpec((tm, WIDTH), row),
            pl.BlockSpec((tm, WIDTH), row),
            pl.BlockSpec((tm, WIDTH), lambda i: (i, N_SEG - 1)),
            pl.BlockSpec((tm, D_MODEL), row),
            pl.BlockSpec((1, WIDTH), fixed),
            pl.BlockSpec((1, WIDTH), fixed),
            pl.BlockSpec((2 * WIDTH, D_MODEL), fixed),
            pl.BlockSpec((1, D_MODEL), fixed),
        ],
        out_specs=[pl.BlockSpec((tm, D_MODEL), row), pl.BlockSpec((tm, D_MODEL), row)],
        out_shape=[jax.ShapeDtypeStruct((lp, D_MODEL), F32), jax.ShapeDtypeStruct((lp, D_MODEL), BF16)],
        compiler_params=pltpu.CompilerParams(
            dimension_semantics=("parallel",), vmem_limit_bytes=VMEM_LIMIT),
        name="mix",
    )(sb, fox, qkv, h, gs, gf, w_out, gn)


def _top16(s):
    tb = s.shape[1]
    rows = lax.broadcasted_iota(jnp.int32, (TOPK, tb), 0)
    out = jnp.full((TOPK, tb), -jnp.inf, F32)
    for r in range(TOPK):
        mx = jnp.max(s, axis=0, keepdims=True)
        out = jnp.where(rows == r, mx, out)
        s = jnp.where(s == mx, -jnp.inf, s)
    return out


def _route_kernel(hf_ref, wq_ref, sk_ref, th_ref, e1_ref, s2_ref, e2_ref, q_scr):
    tb = hf_ref.shape[0]
    q_scr[...] = _nt_dot(wq_ref[...], hf_ref[...]).astype(BF16)
    rows8 = lax.broadcasted_iota(jnp.int32, (8, tb), 0)

    def head(hh, _):
        r1 = pl.multiple_of(hh * 2 * N_KEYS, N_KEYS)
        r2 = pl.multiple_of(hh * 2 * N_KEYS + N_KEYS, N_KEYS)
        s1 = jnp.dot(sk_ref[2 * hh], q_scr[pl.ds(r1, N_KEYS), :], preferred_element_type=F32)
        s2 = jnp.dot(sk_ref[2 * hh + 1], q_scr[pl.ds(r2, N_KEYS), :], preferred_element_type=F32)
        v1 = _top16(s1)
        v2 = _top16(s2)
        cands = [v1 + v2[0:1]]
        for b in range(1, 8):
            cands.append(jnp.where(rows8 < TOPK // (b + 1), v1[0:8] + v2[b:b + 1], -jnp.inf))
        cands.append(v1[0:1] + v2[8:16])
        c = jnp.concatenate(cands, axis=0)
        top = v1[0:1] + v2[0:1]
        zsum = jnp.zeros((1, tb), F32)
        tau = top
        for _r in range(TOPK):
            tau = jnp.max(c, axis=0, keepdims=True)
            zsum = zsum + jnp.exp(tau - top)
            c = jnp.where(c == tau, -jnp.inf, c)
        th = jnp.full((N_KEYS, tb), jnp.inf, F32)
        for b in range(TOPK):
            vb = v2[b:b + 1]
            th = jnp.where(s1 + vb >= tau, vb, th)
        th_ref[hh] = th
        e1_ref[hh] = jnp.exp(s1 - v1[0:1]) / zsum
        s2_ref[hh] = s2
        e2_ref[hh] = jnp.exp(s2 - v2[0:1])
        return 0

    lax.fori_loop(0, PEER_HEADS, head, 0)


def _route(hf, wq_t, sk):
    lp = hf.shape[0]
    tb = PEER_TB
    oshape = jax.ShapeDtypeStruct((PEER_HEADS, N_KEYS, lp), F32)
    ospec = pl.BlockSpec((PEER_HEADS, N_KEYS, tb), lambda t: (0, 0, t))
    return pl.pallas_call(
        _route_kernel,
        grid=(lp // tb,),
        in_specs=[
            pl.BlockSpec((tb, D_MODEL), lambda t: (t, 0)),
            pl.BlockSpec((D_MODEL, D_MODEL), lambda t: (0, 0)),
            pl.BlockSpec((2 * PEER_HEADS, N_KEYS, N_KEYS), lambda t: (0, 0, 0)),
        ],
        out_specs=[ospec] * 4,
        out_shape=[oshape] * 4,
        scratch_shapes=[pltpu.VMEM((D_MODEL, tb), BF16)],
        compiler_params=pltpu.CompilerParams(
            dimension_semantics=("parallel",), vmem_limit_bytes=VMEM_LIMIT),
        name="route",
    )(hf, wq_t, sk)


def _peer_kernel(hf_ref, u_ref, vt_ref, th_ref, e1_ref, s2_ref, e2_ref, o_ref, acc_scr):
    c = pl.program_id(1)
    tb = hf_ref.shape[0]
    ec = u_ref.shape[0]
    ni = ec // N_KEYS

    @pl.when(c == 0)
    def _():
        acc_scr[...] = jnp.zeros_like(acc_scr)

    a = _nt_dot(u_ref[...], hf_ref[...])
    act = 0.5 * a * (1.0 + lax.erf(a * math.sqrt(0.5)))
    gates = []
    for ii in range(ni):
        i = c * ni + ii
        g = jnp.zeros((N_KEYS, tb), F32)
        for hh in range(PEER_HEADS):
            th = th_ref[hh, pl.ds(i, 1), :]
            e1 = e1_ref[hh, pl.ds(i, 1), :]
            g = jnp.where(s2_ref[hh] >= th, g + e2_ref[hh] * e1, g)
        gates.append(g)
    w = (jnp.concatenate(gates, axis=0) * act).astype(BF16)
    acc_scr[...] += jnp.dot(vt_ref[...], w, preferred_element_type=F32)

    @pl.when(c == pl.num_programs(1) - 1)
    def _():
        o_ref[...] = acc_scr[...]


def _peer(hf, u, vt, th, e1, s2, e2):
    lp = hf.shape[0]
    tb, ec = PEER_TB, PEER_EC
    rspec = pl.BlockSpec((PEER_HEADS, N_KEYS, tb), lambda t, c: (0, 0, t))
    return pl.pallas_call(
        _peer_kernel,
        grid=(lp // tb, N_EXPERTS // ec),
        in_specs=[
            pl.BlockSpec((tb, D_MODEL), lambda t, c: (t, 0)),
            pl.BlockSpec((ec, D_MODEL), lambda t, c: (c, 0)),
            pl.BlockSpec((D_MODEL, ec), lambda t, c: (0, c)),
            rspec, rspec, rspec, rspec,
        ],
        out_specs=pl.BlockSpec((D_MODEL, tb), lambda t, c: (0, t)),
        out_shape=jax.ShapeDtypeStruct((D_MODEL, lp), F32),
        scratch_shapes=[pltpu.VMEM((D_MODEL, tb), F32)],
        compiler_params=pltpu.CompilerParams(
            dimension_semantics=("parallel", "arbitrary"), vmem_limit_bytes=VMEM_LIMIT),
        name="peer",
    )(hf, u, vt, th, e1, s2, e2)


def _final_kernel(h1_ref, yt_ref, o_ref):
    o_ref[...] = h1_ref[...] + yt_ref[...].T


def _final(h1, yt):
    lp = h1.shape[0]
    tm = ROW_BLOCK
    return pl.pallas_call(
        _final_kernel,
        grid=(lp // tm,),
        in_specs=[pl.BlockSpec((tm, D_MODEL), lambda i: (i, 0)),
                  pl.BlockSpec((D_MODEL, tm), lambda i: (0, i))],
        out_specs=pl.BlockSpec((tm, D_MODEL), lambda i: (i, 0)),
        out_shape=jax.ShapeDtypeStruct((lp, D_MODEL), F32),
        compiler_params=pltpu.CompilerParams(
            dimension_semantics=("parallel",), vmem_limit_bytes=VMEM_LIMIT),
        name="final",
    )(h1, yt)


def kernel(x, meta_tokens, norm_mix, w_in, b_forget, fox_q_gain, fox_k_gain, sb_out_gain, fox_out_gain,
           w_out, norm_ffn, peer_w_query, peer_sub_keys, peer_u, peer_v):
    batch, seq, d = x.shape
    assert batch == 1 and d == D_MODEL and norm_mix.shape[0] == 1
    used = seq + N_META
    lp = -(-used // ROW_BLOCK) * ROW_BLOCK
    h = jnp.concatenate([meta_tokens.astype(x.dtype), x[0], jnp.zeros((lp - used, d), x.dtype)], axis=0)

    c_main = N_SEG * WIDTH
    w_main = w_in[0][:, :c_main].astype(BF16)
    w_f = jnp.pad(w_in[0][:, c_main:], ((0, 0), (0, LANES - N_HEADS))).astype(BF16)
    b_f = jnp.pad(b_forget[0], (0, LANES - N_HEADS))[None, :]
    qkv, cumf = _proj(h, norm_mix, w_main, w_f, b_f, fox_q_gain, fox_k_gain)

    sb = _sb_attention(qkv, lp)
    zb = (1.02 * HEAD_DIM * SCALE * jnp.max(jnp.abs(fox_q_gain)) * jnp.max(jnp.abs(fox_k_gain))).reshape(1)
    fox = _fox_attention(qkv, cumf, zb, lp)

    h1, hf = _mix(sb, fox, qkv, h, sb_out_gain, fox_out_gain, w_out[0].astype(BF16), norm_ffn)

    wq_t = peer_w_query[0].T.astype(BF16)
    sk = peer_sub_keys[0].reshape(2 * PEER_HEADS, N_KEYS, N_KEYS).astype(BF16)
    th, e1, s2, e2 = _route(hf, wq_t, sk)
    yt = _peer(hf, peer_u[0].astype(BF16), peer_v[0].T.astype(BF16), th, e1, s2, e2)
    out = _final(h1, yt)
    return out[None, N_META:N_META + seq]
```

```python
import functools
import math

import numpy as np
import jax
import jax.numpy as jnp
from jax import lax
from jax.experimental import pallas as pl
from jax.experimental.pallas import tpu as pltpu

F32 = jnp.float32
BF16 = jnp.bfloat16

D_MODEL = 2048
N_META = 16
HEAD_DIM = 128
N_HEADS = 8
WIDTH = N_HEADS * HEAD_DIM
N_SEG = 7
PEER_HEADS = 8
TOPK = 16
N_KEYS = 128
N_EXPERTS = N_KEYS * N_KEYS
EPS = 1e-6
SCALE = 1.0 / math.sqrt(HEAD_DIM)

LANES = 128
ROW_BLOCK = 512
SB_TQ = 512
SB_HG = 2
FOX_TQ = 512
FOX_TK = 512
PEER_TB = 512
PEER_EC = 1024
VMEM_LIMIT = 56 * 1024 * 1024
ZERO_EXP = 110.0
NEG_BIG = -1e30


def _nt_dot(a, b):
    return lax.dot_general(a, b, (((1,), (1,)), ((), ())), preferred_element_type=F32)


def _log_sigmoid(x):
    return jnp.minimum(x, 0.0) - jnp.log(1.0 + jnp.exp(-jnp.abs(x)))


def _softplus(x):
    return jnp.maximum(x, 0.0) + jnp.log(1.0 + jnp.exp(-jnp.abs(x)))


def _proj_kernel(h_ref, g_ref, w_ref, wf_ref, bf_ref, gq_ref, gk_ref, tri_ref,
                 qkv_ref, cumf_ref, hn_scr, carry_scr):
    i = pl.program_id(0)
    j = pl.program_id(1)
    tm = h_ref.shape[0]

    @pl.when(j == 0)
    def _():
        h = h_ref[...]
        ms = jnp.mean(h * h, axis=-1, keepdims=True)
        hn = ((h * lax.rsqrt(ms + EPS)) * g_ref[...]).astype(BF16)
        hn_scr[...] = hn
        lf = _log_sigmoid(jnp.dot(hn, wf_ref[...], preferred_element_type=F32) + bf_ref[...])
        p0 = lf.astype(BF16)
        r0 = lf - p0.astype(F32)
        p1 = r0.astype(BF16)
        p2 = (r0 - p1.astype(F32)).astype(BF16)
        tri = tri_ref[...]
        c = (jnp.dot(tri, p0, preferred_element_type=F32)
             + jnp.dot(tri, p1, preferred_element_type=F32)
             + jnp.dot(tri, p2, preferred_element_type=F32))

        @pl.when(i == 0)
        def _():
            carry_scr[...] = jnp.zeros_like(carry_scr)

        c = c + carry_scr[0:1, :]
        cumf_ref[...] = c
        carry_scr[...] = jnp.broadcast_to(c[tm - 1:tm, :], carry_scr.shape)

    acc = jnp.dot(hn_scr[...], w_ref[...], preferred_element_type=F32)

    @pl.when(j == 0)
    def _():
        qkv_ref[...] = (acc * SCALE).astype(BF16)

    @pl.when((j == 1) | (j == 2) | (j == 5))
    def _():
        qkv_ref[...] = acc.astype(BF16)

    def _qk_norm(gain, scale):
        for k in range(N_HEADS):
            a = acc[:, k * HEAD_DIM:(k + 1) * HEAD_DIM]
            ms = jnp.mean(a * a, axis=-1, keepdims=True)
            y = (a * lax.rsqrt(ms + EPS)) * gain
            if scale != 1.0:
                y = y * scale
            qkv_ref[:, k * HEAD_DIM:(k + 1) * HEAD_DIM] = y.astype(BF16)

    @pl.when(j == 3)
    def _():
        _qk_norm(gq_ref[...], SCALE)

    @pl.when(j == 4)
    def _():
        _qk_norm(gk_ref[...], 1.0)

    @pl.when(j == 6)
    def _():
        qkv_ref[...] = (1.0 / (1.0 + jnp.exp(-acc))).astype(BF16)


def _proj(h, g, w_main, w_f, b_f, gq, gk):
    lp = h.shape[0]
    tm = ROW_BLOCK
    tri = jnp.asarray(np.tril(np.ones((tm, tm), np.float32)), BF16)
    return pl.pallas_call(
        _proj_kernel,
        grid=(lp // tm, N_SEG),
        in_specs=[
            pl.BlockSpec((tm, D_MODEL), lambda i, j: (i, 0)),
            pl.BlockSpec((1, D_MODEL), lambda i, j: (0, 0)),
            pl.BlockSpec((D_MODEL, WIDTH), lambda i, j: (0, j)),
            pl.BlockSpec((D_MODEL, LANES), lambda i, j: (0, 0)),
            pl.BlockSpec((1, LANES), lambda i, j: (0, 0)),
            pl.BlockSpec((1, HEAD_DIM), lambda i, j: (0, 0)),
            pl.BlockSpec((1, HEAD_DIM), lambda i, j: (0, 0)),
            pl.BlockSpec((tm, tm), lambda i, j: (0, 0)),
        ],
        out_specs=[
            pl.BlockSpec((tm, WIDTH), lambda i, j: (i, j)),
            pl.BlockSpec((tm, LANES), lambda i, j: (i, 0)),
        ],
        out_shape=[
            jax.ShapeDtypeStruct((lp, N_SEG * WIDTH), BF16),
            jax.ShapeDtypeStruct((lp, LANES), F32),
        ],
        scratch_shapes=[pltpu.VMEM((tm, D_MODEL), BF16), pltpu.VMEM((8, LANES), F32)],
        compiler_params=pltpu.CompilerParams(
            dimension_semantics=("arbitrary", "arbitrary"), vmem_limit_bytes=VMEM_LIMIT),
        name="proj",
    )(h, g, w_main, w_f, b_f, gq, gk, tri)


def _sb_kernel(q_ref, k_ref, v_ref, u2_ref, o_ref):
    qi = pl.program_id(1)
    tq = q_ref.shape[0]
    nd = tq // LANES
    u2 = u2_ref[...]
    heads = range(SB_HG)
    hcols = [slice(g * HEAD_DIM, (g + 1) * HEAD_DIM) for g in heads]
    q = [q_ref[:, hcols[g]] for g in heads]

    def tile(g, kb, qg, carry, acc, r0):
        start = pl.multiple_of(kb * LANES, LANES)
        k = k_ref[pl.ds(start, LANES), hcols[g]]
        v = v_ref[pl.ds(start, LANES), hcols[g]]
        z = _nt_dot(qg, k)
        sp = _softplus(z)
        if r0 is not None:
            rows = z.shape[0]
            row = lax.broadcasted_iota(jnp.int32, (rows, LANES), 0) + r0
            col = lax.broadcasted_iota(jnp.int32, (rows, LANES), 1) + (start - qi * tq)
            vis = col < row
            sp = jnp.where(vis, sp, 0.0)
        hi = sp.astype(BF16)
        lo = (sp - hi.astype(F32)).astype(BF16)
        cs = jnp.dot(jnp.concatenate([hi, lo], axis=1), u2, preferred_element_type=F32)
        w = jnp.exp(z - cs[:, :LANES] - carry)
        if r0 is not None:
            w = jnp.where(vis, w, 0.0)
        acc = acc + jnp.dot(w.astype(BF16), v, preferred_element_type=F32)
        return carry + cs[:, LANES:], acc

    carry = [jnp.zeros((tq, LANES), F32) for _ in heads]
    acc = [jnp.zeros((tq, HEAD_DIM), F32) for _ in heads]
    for d in range(nd):
        r0 = (nd - 1 - d) * LANES
        for g in heads:
            c, a = tile(g, qi * nd + (nd - 1 - d), q[g][r0:], carry[g][r0:], acc[g][r0:], r0)
            carry[g] = c if r0 == 0 else jnp.concatenate([carry[g][:r0], c], axis=0)
            acc[g] = a if r0 == 0 else jnp.concatenate([acc[g][:r0], a], axis=0)

    def low(cs):
        return functools.reduce(jnp.minimum, [jnp.min(c) for c in cs])

    def cond(s):
        return (s[0] >= 0) & (s[3] < ZERO_EXP)

    def body(s):
        out = [tile(g, s[0], q[g], s[1][g], s[2][g], None) for g in heads]
        cs = [o[0] for o in out]
        return s[0] - 1, cs, [o[1] for o in out], low(cs)

    s = lax.while_loop(cond, body, (qi * nd - 1, carry, acc, low(carry)))
    for g in heads:
        o_ref[:, hcols[g]] = s[2][g]


def _sb_attention(qkv, lp):
    tq, wg = SB_TQ, SB_HG * HEAD_DIM
    ng = N_HEADS // SB_HG
    tri = np.tril(np.ones((LANES, LANES), np.float32))
    half = np.concatenate([tri, np.ones((LANES, LANES), np.float32)], axis=1)
    u2 = jnp.asarray(np.concatenate([half, half], axis=0), BF16)
    return pl.pallas_call(
        _sb_kernel,
        grid=(ng, lp // tq),
        in_specs=[
            pl.BlockSpec((tq, wg), lambda h, i: (i, h)),
            pl.BlockSpec((lp, wg), lambda h, i: (0, ng + h)),
            pl.BlockSpec((lp, wg), lambda h, i: (0, 2 * ng + h)),
            pl.BlockSpec((2 * LANES, 2 * LANES), lambda h, i: (0, 0)),
        ],
        out_specs=pl.BlockSpec((tq, wg), lambda h, i: (i, h)),
        out_shape=jax.ShapeDtypeStruct((lp, WIDTH), F32),
        compiler_params=pltpu.CompilerParams(
            dimension_semantics=("arbitrary", "arbitrary"), vmem_limit_bytes=VMEM_LIMIT),
        name="sb_attn",
    )(qkv, qkv, qkv, u2)


def _fox_kernel(cend_ref, zb_ref, q_ref, k_ref, v_ref, cq_ref, ck_ref, o_ref):
    h = pl.program_id(0)
    qi = pl.program_id(1)
    tq = q_ref.shape[0]
    tk = FOX_TK
    nc = tk // LANES
    q = q_ref[...]
    cq = cq_ref[0]
    row = lax.broadcasted_iota(jnp.int32, (tq, tk), 0) + qi * tq
    col = lax.broadcasted_iota(jnp.int32, (tq, tk), 1)

    def tile(kt, m, l, acc, masked):
        start = pl.multiple_of(kt * tk, tk)
        k = k_ref[pl.ds(start, tk), :]
        v = v_ref[pl.ds(start, tk), :]
        z = _nt_dot(q, k)
        s = jnp.concatenate(
            [z[:, c * LANES:(c + 1) * LANES] - ck_ref[0, pl.ds(kt * nc + c, 1), :] for c in range(nc)],
            axis=1) + cq
        if masked:
            s = jnp.where((col + start) <= row, s, NEG_BIG)
        m_new = jnp.maximum(m, jnp.max(s, axis=1, keepdims=True))
        alpha = jnp.exp(m - m_new)
        p = jnp.exp(s - m_new)
        l = alpha * l + jnp.sum(p, axis=1, keepdims=True)
        acc = acc * alpha + jnp.dot(p.astype(BF16), v, preferred_element_type=F32)
        return m_new, l, acc

    m = jnp.full((tq, 1), NEG_BIG, F32)
    l = jnp.zeros((tq, 1), F32)
    acc = jnp.zeros((tq, HEAD_DIM), F32)
    kd = (qi * tq) // tk
    m, l, acc = tile(kd, m, l, acc, True)

    zb = zb_ref[0]

    def slack(m):
        return jnp.max(zb + cq - m)

    def cond(s):
        kt = s[0]
        return (kt >= 0) & (s[4] - cend_ref[h, jnp.maximum(kt, 0)] > -ZERO_EXP)

    def body(s):
        m, l, acc = tile(s[0], s[1], s[2], s[3], False)
        return s[0] - 1, m, l, acc, slack(m)

    s = lax.while_loop(cond, body, (kd - 1, m, l, acc, slack(m)))
    o_ref[...] = s[3] / s[2]


def _fox_attention(qkv, cumf, zb, lp):
    tq, tk = FOX_TQ, FOX_TK
    cf = cumf[:, :N_HEADS].T
    cq = cf[:, :, None]
    ck = cf.reshape(N_HEADS, lp // LANES, LANES)
    cend = cf[:, tk - 1::tk]
    grid_spec = pltpu.PrefetchScalarGridSpec(
        num_scalar_prefetch=2,
        grid=(N_HEADS, lp // tq),
        in_specs=[
            pl.BlockSpec((tq, HEAD_DIM), lambda h, i, *_: (i, 3 * N_HEADS + h)),
            pl.BlockSpec((lp, HEAD_DIM), lambda h, i, *_: (0, 4 * N_HEADS + h)),
            pl.BlockSpec((lp, HEAD_DIM), lambda h, i, *_: (0, 5 * N_HEADS + h)),
            pl.BlockSpec((1, tq, 1), lambda h, i, *_: (h, i, 0)),
            pl.BlockSpec((1, lp // LANES, LANES), lambda h, i, *_: (h, 0, 0)),
        ],
        out_specs=pl.BlockSpec((tq, HEAD_DIM), lambda h, i, *_: (i, h)),
    )
    return pl.pallas_call(
        _fox_kernel,
        grid_spec=grid_spec,
        out_shape=jax.ShapeDtypeStruct((lp, WIDTH), F32),
        compiler_params=pltpu.CompilerParams(
            dimension_semantics=("arbitrary", "arbitrary"), vmem_limit_bytes=VMEM_LIMIT),
        name="fox_attn",
    )(cend, zb, qkv, qkv, qkv, cq, ck)


def _mix_kernel(sb_ref, fox_ref, gate_ref, h_ref, gs_ref, gf_ref, w_ref, gn_ref, h1_ref, hf_ref):
    def norm(x, g):
        ms = jnp.mean(x * x, axis=-1, keepdims=True)
        return (x * lax.rsqrt(ms + EPS)) * g

    a = norm(sb_ref[...], gs_ref[...]).astype(BF16)
    b = (norm(fox_ref[...], gf_ref[...]) * gate_ref[...].astype(F32)).astype(BF16)
    y = (jnp.dot(a, w_ref[0:WIDTH, :], preferred_element_type=F32)
         + jnp.dot(b, w_ref[WIDTH:2 * WIDTH, :], preferred_element_type=F32))
    h1 = h_ref[...] + y
    h1_ref[...] = h1
    hf_ref[...] = norm(h1, gn_ref[...]).T.astype(BF16)


def _mix(sb, fox, qkv, h, gs, gf, w_out, gn):
    lp = h.shape[0]
    tm = ROW_BLOCK
    row = lambda i: (i, 0)
    fixed = lambda i: (0, 0)
    return pl.pallas_call(
        _mix_kernel,
        grid=(lp // tm,),
        in_specs=[
            pl.BlockSpec((tm, WIDTH), row),
            pl.BlockSpec((tm, WIDTH), row),
            pl.BlockSpec((tm, WIDTH), lambda i: (i, N_SEG - 1)),
            pl.BlockSpec((tm, D_MODEL), row),
            pl.BlockSpec((1, WIDTH), fixed),
            pl.BlockSpec((1, WIDTH), fixed),
            pl.BlockSpec((2 * WIDTH, D_MODEL), fixed),
            pl.BlockSpec((1, D_MODEL), fixed),
        ],
        out_specs=[pl.BlockSpec((tm, D_MODEL), row), pl.BlockSpec((D_MODEL, tm), lambda i: (0, i))],
        out_shape=[jax.ShapeDtypeStruct((lp, D_MODEL), F32), jax.ShapeDtypeStruct((D_MODEL, lp), BF16)],
        compiler_params=pltpu.CompilerParams(
            dimension_semantics=("parallel",), vmem_limit_bytes=VMEM_LIMIT),
        name="mix",
    )(sb, fox, qkv, h, gs, gf, w_out, gn)


def _top16(s):
    tb = s.shape[1]
    rows = lax.broadcasted_iota(jnp.int32, (TOPK, tb), 0)
    out = jnp.full((TOPK, tb), -jnp.inf, F32)
    for r in range(TOPK):
        mx = jnp.max(s, axis=0, keepdims=True)
        out = jnp.where(rows == r, mx, out)
        s = jnp.where(s == mx, -jnp.inf, s)
    return out


def _route_kernel(hf_ref, wq_ref, sk_ref, th_ref, e1_ref, s2_ref, e2_ref, q_scr):
    tb = hf_ref.shape[1]
    q_scr[...] = jnp.dot(wq_ref[...], hf_ref[...], preferred_element_type=F32).astype(BF16)
    rows8 = lax.broadcasted_iota(jnp.int32, (8, tb), 0)

    def head(hh, _):
        r1 = pl.multiple_of(hh * 2 * N_KEYS, N_KEYS)
        r2 = pl.multiple_of(hh * 2 * N_KEYS + N_KEYS, N_KEYS)
        s1 = jnp.dot(sk_ref[2 * hh], q_scr[pl.ds(r1, N_KEYS), :], preferred_element_type=F32)
        s2 = jnp.dot(sk_ref[2 * hh + 1], q_scr[pl.ds(r2, N_KEYS), :], preferred_element_type=F32)
        v1 = _top16(s1)
        v2 = _top16(s2)
        cands = [v1 + v2[0:1]]
        for b in range(1, 8):
            cands.append(jnp.where(rows8 < TOPK // (b + 1), v1[0:8] + v2[b:b + 1], -jnp.inf))
        cands.append(v1[0:1] + v2[8:16])
        c = jnp.concatenate(cands, axis=0)
        top = v1[0:1] + v2[0:1]
        zsum = jnp.zeros((1, tb), F32)
        tau = top
        for _r in range(TOPK):
            tau = jnp.max(c, axis=0, keepdims=True)
            zsum = zsum + jnp.exp(tau - top)
            c = jnp.where(c == tau, -jnp.inf, c)
        th = jnp.full((N_KEYS, tb), jnp.inf, F32)
        for b in range(TOPK):
            vb = v2[b:b + 1]
            th = jnp.where(s1 + vb >= tau, vb, th)
        th_ref[hh] = th
        e1_ref[hh] = jnp.exp(s1 - v1[0:1]) / zsum
        s2_ref[hh] = s2
        e2_ref[hh] = jnp.exp(s2 - v2[0:1])
        return 0

    lax.fori_loop(0, PEER_HEADS, head, 0)


def _route(hf, wq_t, sk):
    lp = hf.shape[1]
    tb = PEER_TB
    oshape = jax.ShapeDtypeStruct((PEER_HEADS, N_KEYS, lp), F32)
    ospec = pl.BlockSpec((PEER_HEADS, N_KEYS, tb), lambda t: (0, 0, t))
    return pl.pallas_call(
        _route_kernel,
        grid=(lp // tb,),
        in_specs=[
            pl.BlockSpec((D_MODEL, tb), lambda t: (0, t)),
            pl.BlockSpec((D_MODEL, D_MODEL), lambda t: (0, 0)),
            pl.BlockSpec((2 * PEER_HEADS, N_KEYS, N_KEYS), lambda t: (0, 0, 0)),
        ],
        out_specs=[ospec] * 4,
        out_shape=[oshape] * 4,
        scratch_shapes=[pltpu.VMEM((D_MODEL, tb), BF16)],
        compiler_params=pltpu.CompilerParams(
            dimension_semantics=("parallel",), vmem_limit_bytes=VMEM_LIMIT),
        name="route",
    )(hf, wq_t, sk)


def _peer_kernel(hf_ref, u_ref, vt_ref, th_ref, e1_ref, s2_ref, e2_ref, h1_ref, o_ref, acc_scr, a_scr, w_scr):
    c = pl.program_id(1)
    tb = hf_ref.shape[1]
    ni = u_ref.shape[0] // N_KEYS

    @pl.when(c == 0)
    def _():
        acc_scr[...] = jnp.zeros_like(acc_scr)

    a_scr[...] = jnp.dot(u_ref[...], hf_ref[...], preferred_element_type=F32)
    for st in range(tb // LANES):
        cols = slice(st * LANES, (st + 1) * LANES)
        th8 = [th_ref[hh, :, cols] for hh in range(PEER_HEADS)]
        e18 = [e1_ref[hh, :, cols] for hh in range(PEER_HEADS)]
        for ii in range(ni):
            rows = slice(ii * N_KEYS, (ii + 1) * N_KEYS)
            g = jnp.zeros((N_KEYS, LANES), F32)
            for hh in range(PEER_HEADS):
                g = jnp.where(s2_ref[hh, :, cols] >= th8[hh][ii:ii + 1, :],
                              g + e2_ref[hh, :, cols] * e18[hh][ii:ii + 1, :], g)
            a = a_scr[rows, cols]
            act = 0.5 * a * (1.0 + lax.erf(a * math.sqrt(0.5)))
            w_scr[rows, cols] = (g * act).astype(BF16)
    acc_scr[...] += jnp.dot(vt_ref[...], w_scr[...], preferred_element_type=F32)

    @pl.when(c == pl.num_programs(1) - 1)
    def _():
        o_ref[...] = h1_ref[...] + acc_scr[...].T


def _peer(hf, u, vt, th, e1, s2, e2, h1):
    lp = hf.shape[1]
    tb, ec = PEER_TB, PEER_EC
    ne = N_EXPERTS // ec
    rspec = pl.BlockSpec((PEER_HEADS, N_KEYS, tb), lambda t, c: (0, 0, t))
    cspec = pl.BlockSpec((PEER_HEADS, ec // N_KEYS, tb), lambda t, c: (0, c, t))
    return pl.pallas_call(
        _peer_kernel,
        grid=(lp // tb, ne),
        in_specs=[
            pl.BlockSpec((D_MODEL, tb), lambda t, c: (0, t)),
            pl.BlockSpec((ec, D_MODEL), lambda t, c: (c, 0)),
            pl.BlockSpec((D_MODEL, ec), lambda t, c: (0, c)),
            cspec, cspec, rspec, rspec,
            pl.BlockSpec((tb, D_MODEL), lambda t, c: (t, 0), pipeline_mode=pl.Buffered(1)),
        ],
        out_specs=pl.BlockSpec((tb, D_MODEL), lambda t, c: (t, 0)),
        out_shape=jax.ShapeDtypeStruct((lp, D_MODEL), F32),
        scratch_shapes=[pltpu.VMEM((D_MODEL, tb), F32), pltpu.VMEM((ec, tb), F32), pltpu.VMEM((ec, tb), BF16)],
        compiler_params=pltpu.CompilerParams(
            dimension_semantics=("parallel", "arbitrary"), vmem_limit_bytes=VMEM_LIMIT),
        name="peer",
    )(hf, u, vt, th, e1, s2, e2, h1)


def kernel(x, meta_tokens, norm_mix, w_in, b_forget, fox_q_gain, fox_k_gain, sb_out_gain, fox_out_gain,
           w_out, norm_ffn, peer_w_query, peer_sub_keys, peer_u, peer_v):
    batch, seq, d = x.shape
    assert batch == 1 and d == D_MODEL and norm_mix.shape[0] == 1
    used = seq + N_META
    lp = -(-used // ROW_BLOCK) * ROW_BLOCK
    h = jnp.concatenate([meta_tokens.astype(x.dtype), x[0], jnp.zeros((lp - used, d), x.dtype)], axis=0)

    c_main = N_SEG * WIDTH
    w_main = w_in[0][:, :c_main].astype(BF16)
    w_f = jnp.pad(w_in[0][:, c_main:], ((0, 0), (0, LANES - N_HEADS))).astype(BF16)
    b_f = jnp.pad(b_forget[0], (0, LANES - N_HEADS))[None, :]
    qkv, cumf = _proj(h, norm_mix, w_main, w_f, b_f, fox_q_gain, fox_k_gain)

    sb = _sb_attention(qkv, lp)
    zb = (1.02 * HEAD_DIM * SCALE * jnp.max(jnp.abs(fox_q_gain)) * jnp.max(jnp.abs(fox_k_gain))).reshape(1)
    fox = _fox_attention(qkv, cumf, zb, lp)

    h1, hf = _mix(sb, fox, qkv, h, sb_out_gain, fox_out_gain, w_out[0].astype(BF16), norm_ffn)

    wq_t = peer_w_query[0].T.astype(BF16)
    sk = peer_sub_keys[0].reshape(2 * PEER_HEADS, N_KEYS, N_KEYS).astype(BF16)
    th, e1, s2, e2 = _route(hf, wq_t, sk)
    out = _peer(hf, peer_u[0].astype(BF16), peer_v[0].T.astype(BF16), th, e1, s2, e2, h1)
    return out[None, N_META:N_META + seq]
```
